```python
import jax, jax.numpy as jnp
from jax import lax
import numpy as np

D_MODEL = 1024
BATCH = 1
SEQ = 16384
DEPTH = 1

MLA_HEADS = 8
MLA_Q_RANK = 384
MLA_KV_RANK = 256
MLA_NOPE = 128
MLA_ROPE = 64
MLA_V = 128
MLA_WIDTH = MLA_HEADS * MLA_V
RET_HEADS = 4
RET_QK = 256
RET_V = 512
RET_WIDTH = RET_HEADS * RET_V
RET_CHUNK = 128
ATTN_BLOCK = 128
ROPE_THETA = 10000.0
EPS = 1e-6
IN_SIZES = (MLA_Q_RANK, MLA_KV_RANK, MLA_ROPE, MLA_WIDTH,
            RET_HEADS * RET_QK, RET_HEADS * RET_QK, RET_WIDTH, RET_WIDTH,
            D_MODEL, D_MODEL)
IN_TOTAL = 384 + 256 + 64 + 1024 + 1024 + 1024 + 2048 + 2048 + 1024 + 1024

kernel_name = "hybrid_mla_retention_gated_block"


def _split_points():
    pts, acc = [], 0
    for s in IN_SIZES[:-1]:
        acc += s
        pts.append(acc)
    return pts


def rmsnorm(x, g):
    xf = x.astype(jnp.float32)
    r = xf * lax.rsqrt(jnp.mean(xf * xf, axis=-1, keepdims=True) + EPS)
    return (r * g.astype(jnp.float32)).astype(x.dtype)


def rotary(x, positions):
    d = x.shape[-1]
    inv_freq = ROPE_THETA ** (-jnp.arange(0, d, 2, dtype=jnp.float32) / d)
    ang = positions.astype(jnp.float32)[..., None] * inv_freq
    cos = jnp.cos(ang)[:, :, None, :]
    sin = jnp.sin(ang)[:, :, None, :]
    xf = x.astype(jnp.float32)
    x1, x2 = xf[..., : d // 2], xf[..., d // 2:]
    out = jnp.concatenate([x1 * cos - x2 * sin, x2 * cos + x1 * sin], axis=-1)
    return out.astype(x.dtype)


def causal_block_attention(q, k, v):
    B, S, H, Dh = q.shape
    nb = S // ATTN_BLOCK
    scale = Dh ** -0.5
    qb = q.reshape(B, nb, ATTN_BLOCK, H, Dh).transpose(1, 0, 3, 2, 4)
    k_idx = jnp.arange(S)

    def one_block(args):
        qi, i = args
        s = jnp.einsum('bhqd,bkhd->bhqk', qi, k, preferred_element_type=jnp.float32) * scale
        q_idx = i * ATTN_BLOCK + jnp.arange(ATTN_BLOCK)
        mask = k_idx[None, :] <= q_idx[:, None]
        s = jnp.where(mask[None, None], s, -1e30)
        p = jax.nn.softmax(s, axis=-1).astype(v.dtype)
        return jnp.einsum('bhqk,bkhd->bqhd', p, v)

    out = lax.map(one_block, (qb, jnp.arange(nb)))
    return out.transpose(1, 0, 2, 3, 4).reshape(B, S, H, v.shape[-1])


def mla_branch(c_q, c_kv, k_pe, positions, g_q, w_q_up, g_kv, w_kv_up):
    B, S, _ = c_q.shape
    q = (rmsnorm(c_q, g_q) @ w_q_up).reshape(B, S, MLA_HEADS, MLA_NOPE + MLA_ROPE)
    q = jnp.concatenate([q[..., :MLA_NOPE], rotary(q[..., MLA_NOPE:], positions)], axis=-1)
    kv = (rmsnorm(c_kv, g_kv) @ w_kv_up).reshape(B, S, MLA_HEADS, MLA_NOPE + MLA_V)
    k_nope, v = kv[..., :MLA_NOPE], kv[..., MLA_NOPE:]
    k_rope = rotary(k_pe[:, :, None, :], positions)
    k = jnp.concatenate([k_nope, jnp.broadcast_to(k_rope, (B, S, MLA_HEADS, MLA_ROPE))], axis=-1)
    return causal_block_attention(q, k, v).reshape(B, S, MLA_WIDTH)


def retention_branch(q, k, v, positions, g_ret):
    B, S, _ = q.shape
    H, C = RET_HEADS, RET_CHUNK
    n = S // C
    q = rotary(q.reshape(B, S, H, RET_QK), positions)
    k = rotary(k.reshape(B, S, H, RET_QK), positions) * (RET_QK ** -0.5)
    v = v.reshape(B, S, H, RET_V)
    log_gamma = jnp.log1p(-jnp.exp2(-5.0 - jnp.arange(H, dtype=jnp.float32)))
    idx = jnp.arange(C, dtype=jnp.float32)
    diff = idx[:, None] - idx[None, :]
    decay_in = jnp.where(diff >= 0, jnp.exp(log_gamma[:, None, None] * jnp.maximum(diff, 0.0)), 0.0)
    xi = jnp.exp(log_gamma[:, None] * (idx + 1.0))
    zeta = jnp.exp(log_gamma[:, None] * (C - 1.0 - idx))
    gamma_c = jnp.exp(log_gamma * C)

    def to_chunks(t):
        return t.reshape(B, n, C, H, t.shape[-1]).transpose(1, 0, 3, 2, 4).astype(jnp.float32)

    def step(R, chunk):
        qc, kc, vc = chunk
        inner = jnp.einsum('bhqk,bhkd->bhqd', jnp.einsum('bhqd,bhkd->bhqk', qc, kc) * decay_in[None], vc)
        cross = jnp.einsum('bhqd,bhde->bhqe', qc, R) * xi[None, :, :, None]
        R = R * gamma_c[None, :, None, None] + jnp.einsum('bhkd,bhke->bhde', kc * zeta[None, :, :, None], vc)
        return R, inner + cross

    R0 = jnp.zeros((B, H, RET_QK, RET_V), jnp.float32)
    _, out = lax.scan(step, R0, (to_chunks(q), to_chunks(k), to_chunks(v)))
    out = out.transpose(1, 0, 3, 2, 4).reshape(B, S, H, RET_V)
    mu = jnp.mean(out, axis=-1, keepdims=True)
    var = jnp.mean(jnp.square(out - mu), axis=-1, keepdims=True)
    out = ((out - mu) * lax.rsqrt(var + EPS)).reshape(B, S, RET_WIDTH) * g_ret.astype(jnp.float32)
    return out.astype(q.dtype)


def setup_inputs(seed: int = 0) -> dict:
    key = jax.random.key(seed)
    ks = jax.random.split(key, 17)
    f32 = jnp.float32

    def w(k, shape, fan_in):
        return jax.random.normal(k, shape, f32) * (fan_in ** -0.5)

    def gain(k, shape):
        return 1.0 + 0.02 * jax.random.normal(k, shape, f32)

    x = jax.random.normal(ks[0], (BATCH, SEQ, D_MODEL), f32)
    c = jax.random.normal(ks[1], (BATCH, D_MODEL), f32)
    positions = (jnp.arange(SEQ, dtype=jnp.int32)[None, :]
                 + jax.random.randint(ks[2], (BATCH, 1), 0, 1024, dtype=jnp.int32))
    return {
        "x": x,
        "c": c,
        "positions": positions,
        "w_ada": w(ks[3], (DEPTH, D_MODEL, 3 * D_MODEL), D_MODEL),
        "b_ada": 0.02 * jax.random.normal(ks[4], (DEPTH, 3 * D_MODEL), f32),
        "g_pre": gain(ks[5], (DEPTH, D_MODEL)),
        "w_in": w(ks[6], (DEPTH, D_MODEL, IN_TOTAL), D_MODEL),
        "g_q": gain(ks[7], (DEPTH, MLA_Q_RANK)),
        "w_q_up": w(ks[8], (DEPTH, MLA_Q_RANK, MLA_HEADS * (MLA_NOPE + MLA_ROPE)), MLA_Q_RANK),
        "g_kv": gain(ks[9], (DEPTH, MLA_KV_RANK)),
        "w_kv_up": w(ks[10], (DEPTH, MLA_KV_RANK, MLA_HEADS * (MLA_NOPE + MLA_V)), MLA_KV_RANK),
        "g_ret": gain(ks[11], (DEPTH, RET_WIDTH)),
        "w_proj_mla": w(ks[12], (DEPTH, MLA_WIDTH, D_MODEL), MLA_WIDTH),
        "w_proj_ret": w(ks[13], (DEPTH, RET_WIDTH, D_MODEL), RET_WIDTH),
        "w_out": w(ks[14], (DEPTH, D_MODEL, D_MODEL), D_MODEL),
        "g_final": gain(ks[15], (D_MODEL,)),
    }


def reference(x, c, positions, w_ada, b_ada, g_pre, w_in, g_q, w_q_up, g_kv, w_kv_up,
              g_ret, w_proj_mla, w_proj_ret, w_out, g_final):
    pts = _split_points()
    for l in range(DEPTH):
        mod = jax.nn.silu(c) @ w_ada[l] + b_ada[l]
        shift, scale, gate = jnp.split(mod[:, None, :], 3, axis=-1)
        h = rmsnorm(x, g_pre[l]) * (1.0 + scale) + shift
        proj = h @ w_in[l]
        (c_q, c_kv, k_pe, gate_mla, q_ret, k_ret, v_ret, gate_ret,
         merge_mla, merge_ret) = jnp.split(proj, pts, axis=-1)
        y_mla = mla_branch(c_q, c_kv, k_pe, positions, g_q[l], w_q_up[l], g_kv[l], w_kv_up[l])
        y_mla = y_mla * jax.nn.silu(gate_mla)
        y_ret = retention_branch(q_ret, k_ret, v_ret, positions, g_ret[l]) * jax.nn.silu(gate_ret)
        merged = (jax.nn.sigmoid(merge_mla) * (y_mla @ w_proj_mla[l])
                  + jax.nn.sigmoid(merge_ret) * (y_ret @ w_proj_ret[l]))
        x = x + gate * (merged @ w_out[l])
    return rmsnorm(x, g_final)
```

```python
import functools
import math

import numpy as np
import jax
import jax.numpy as jnp
from jax import lax
from jax.experimental import pallas as pl
from jax.experimental.pallas import tpu as pltpu

F32 = jnp.float32
BF16 = jnp.bfloat16

D_MODEL = 1024
MLA_HEADS = 8
MLA_Q_RANK = 384
MLA_KV_RANK = 256
MLA_NOPE = 128
MLA_ROPE = 64
MLA_V = 128
MLA_QK = MLA_NOPE + MLA_ROPE
MLA_QK_PAD = 256
RET_HEADS = 4
RET_QK = 256
RET_V = 512
ROPE_THETA = 10000.0
EPS = 1e-6
NEG = -1e30

ATTN_BLOCK = 512
RET_CHUNK = 256
ROW_TILE = 512
VMEM_LIMIT_BYTES = 56 * 1024 * 1024

_TRANS_B = (((1,), (1,)), ((), ()))
_TRANS_A = (((0,), (0,)), ((), ()))


def _sigmoid(v):
    return 1.0 / (1.0 + jnp.exp(-v))


def _silu(v):
    return v * _sigmoid(v)


def _rms(v, g):
    return v * lax.rsqrt(jnp.mean(v * v, axis=-1, keepdims=True) + EPS) * g


def _params(*sem):
    return pltpu.CompilerParams(dimension_semantics=sem, vmem_limit_bytes=VMEM_LIMIT_BYTES)


def _resident(shape):
    zeros = (0,) * len(shape)
    return pl.BlockSpec(shape, lambda *_: zeros, pipeline_mode=pl.Buffered(1))


def _mod_kernel(c_ref, w_ref, b_ref, o_ref):
    a = _silu(c_ref[...])
    a8 = jnp.broadcast_to(a, (8, a.shape[1])).astype(BF16)
    r = jnp.dot(a8, w_ref[...].astype(BF16), preferred_element_type=F32)
    o_ref[...] = r[0:1] + b_ref[...]


def _mod(c, w_ada, b_ada):
    d, n = w_ada.shape
    tn = 512
    return pl.pallas_call(
        _mod_kernel,
        grid=(n // tn,),
        in_specs=[pl.BlockSpec((1, d), lambda j: (0, 0)),
                  pl.BlockSpec((d, tn), lambda j: (0, j)),
                  pl.BlockSpec((1, tn), lambda j: (0, j))],
        out_specs=pl.BlockSpec((1, tn), lambda j: (0, j)),
        out_shape=jax.ShapeDtypeStruct((1, n), F32),
        name="adaln_mod",
        compiler_params=_params("arbitrary"),
    )(c, w_ada, b_ada.reshape(1, n))


def _hnorm_kernel(x_ref, mod_ref, g_ref, h_ref):
    d = x_ref.shape[1]
    shift = mod_ref[:, 0:d]
    scale = mod_ref[:, d:2 * d]
    h = _rms(x_ref[...], g_ref[...]) * (1.0 + scale) + shift
    h_ref[...] = h.astype(BF16)


def _hnorm(x2, mod, g_pre):
    s, d = x2.shape
    tm = ROW_TILE
    return pl.pallas_call(
        _hnorm_kernel,
        grid=(s // tm,),
        in_specs=[pl.BlockSpec((tm, d), lambda i: (i, 0)),
                  pl.BlockSpec((1, 3 * d), lambda i: (0, 0)),
                  pl.BlockSpec((1, d), lambda i: (0, 0))],
        out_specs=pl.BlockSpec((tm, d), lambda i: (i, 0)),
        out_shape=jax.ShapeDtypeStruct((s, d), BF16),
        name="hnorm",
        compiler_params=_params("arbitrary"),
    )(x2, mod, g_pre.reshape(1, d))


def _rope_kernel(pos_ref, fr_ref, fm_ref, cr_ref, sr_ref, cm_ref, sm_ref):
    p = pos_ref[...].astype(F32)
    ang_r = p * fr_ref[...]
    cr_ref[...] = jnp.cos(ang_r).T
    sr_ref[...] = jnp.sin(ang_r).T
    ang_m = p * fm_ref[...]
    cm_ref[...] = jnp.cos(ang_m)
    sm_ref[...] = jnp.sin(ang_m)


def _inv_freq(dim):
    return ROPE_THETA ** (-jnp.arange(0, dim, 2, dtype=F32) / dim)


def _rope_tables(positions):
    s = positions.shape[1]
    ts = ROW_TILE
    nr, nm = RET_QK // 2, MLA_ROPE // 2
    fr = jnp.broadcast_to(_inv_freq(RET_QK)[:, None], (nr, ts))
    fm = jnp.broadcast_to(_inv_freq(MLA_ROPE)[:, None], (nm, ts))
    return pl.pallas_call(
        _rope_kernel,
        grid=(s // ts,),
        in_specs=[pl.BlockSpec((1, ts), lambda i: (0, i)),
                  pl.BlockSpec((nr, ts), lambda i: (0, 0)),
                  pl.BlockSpec((nm, ts), lambda i: (0, 0))],
        out_specs=[pl.BlockSpec((ts, nr), lambda i: (i, 0)),
                   pl.BlockSpec((ts, nr), lambda i: (i, 0)),
                   pl.BlockSpec((nm, ts), lambda i: (0, i)),
                   pl.BlockSpec((nm, ts), lambda i: (0, i))],
        out_shape=[jax.ShapeDtypeStruct((s, nr), F32), jax.ShapeDtypeStruct((s, nr), F32),
                   jax.ShapeDtypeStruct((nm, s), F32), jax.ShapeDtypeStruct((nm, s), F32)],
        name="rope_tables",
        compiler_params=_params("arbitrary"),
    )(positions, fr, fm)


_Q_SCALE = (MLA_QK ** -0.5) * math.log2(math.e)


def _mlaprep_kernel(h_ref, wc_ref, wpe_ref, gq_ref, gkv_ref, wq_ref, wk_ref, wv_ref,
                    cm_ref, sm_ref, qt_ref, k_ref, vt_ref):
    h = h_ref[...]
    tm = h.shape[0]
    c = jnp.dot(h, wc_ref[...], preferred_element_type=F32)
    cq = _rms(c[:, :MLA_Q_RANK], gq_ref[...]).astype(BF16)
    ckv = _rms(c[:, MLA_Q_RANK:], gkv_ref[...]).astype(BF16)
    qt = lax.dot_general(wq_ref[...], cq, _TRANS_B, preferred_element_type=F32)
    kn = jnp.dot(ckv, wk_ref[...], preferred_element_type=F32)
    vt = lax.dot_general(wv_ref[...], ckv, _TRANS_B, preferred_element_type=F32)
    kpe = lax.dot_general(wpe_ref[...], h, _TRANS_B, preferred_element_type=F32)
    cos = cm_ref[...]
    sin = sm_ref[...]
    half = MLA_ROPE // 2
    x1, x2 = kpe[0:half], kpe[half:MLA_ROPE]
    krope_t = jnp.concatenate([x1 * cos - x2 * sin, x2 * cos + x1 * sin, kpe[MLA_ROPE:]], axis=0)
    krope = krope_t.T.astype(BF16)
    zpad = jnp.zeros((MLA_QK_PAD - MLA_QK, tm), BF16)
    for hh in range(MLA_HEADS):
        b = hh * MLA_QK
        q1 = qt[b + MLA_NOPE:b + MLA_NOPE + half]
        q2 = qt[b + MLA_NOPE + half:b + MLA_QK]
        qt_ref[hh, 0:MLA_NOPE, :] = (qt[b:b + MLA_NOPE] * _Q_SCALE).astype(BF16)
        qt_ref[hh, MLA_NOPE:MLA_NOPE + half, :] = ((q1 * cos - q2 * sin) * _Q_SCALE).astype(BF16)
        qt_ref[hh, MLA_NOPE + half:MLA_QK, :] = ((q2 * cos + q1 * sin) * _Q_SCALE).astype(BF16)
        qt_ref[hh, MLA_QK:MLA_QK_PAD, :] = zpad
        k_ref[hh, :, 0:MLA_NOPE] = kn[:, hh * MLA_NOPE:(hh + 1) * MLA_NOPE].astype(BF16)
        k_ref[hh, :, MLA_NOPE:MLA_QK_PAD] = krope
        vt_ref[hh, 0] = vt[hh * MLA_V:(hh + 1) * MLA_V].astype(BF16)


def _mlaprep(h, wc, wpe_t, g_q, g_kv, wq_t, wk, wv_t, cos_m, sin_m):
    s, d = h.shape
    tm = ATTN_BLOCK
    nb = s // tm
    hq = MLA_HEADS
    return pl.pallas_call(
        _mlaprep_kernel,
        grid=(nb,),
        in_specs=[pl.BlockSpec((tm, d), lambda i: (i, 0)),
                  _resident(wc.shape), _resident(wpe_t.shape),
                  _resident((1, MLA_Q_RANK)), _resident((1, MLA_KV_RANK)),
                  _resident(wq_t.shape), _resident(wk.shape), _resident(wv_t.shape),
                  pl.BlockSpec((MLA_ROPE // 2, tm), lambda i: (0, i)),
                  pl.BlockSpec((MLA_ROPE // 2, tm), lambda i: (0, i))],
        out_specs=[pl.BlockSpec((hq, MLA_QK_PAD, tm), lambda i: (0, 0, i)),
                   pl.BlockSpec((hq, tm, MLA_QK_PAD), lambda i: (0, i, 0)),
                   pl.BlockSpec((hq, 1, MLA_V, tm), lambda i: (0, i, 0, 0))],
        out_shape=[jax.ShapeDtypeStruct((hq, MLA_QK_PAD, s), BF16),
                   jax.ShapeDtypeStruct((hq, s, MLA_QK_PAD), BF16),
                   jax.ShapeDtypeStruct((hq, nb, MLA_V, tm), BF16)],
        name="mla_prep",
        compiler_params=_params("arbitrary"),
    )(h, wc, wpe_t, g_q.reshape(1, -1), g_kv.reshape(1, -1), wq_t, wk, wv_t, cos_m, sin_m)


def _attn_kernel(qt_ref, k_ref, vt_ref, o_ref, m_sc, l_sc, acc_sc):
    i = pl.program_id(1)
    qt = qt_ref[...]
    m_sc[...] = jnp.full(m_sc.shape, NEG, F32)
    l_sc[...] = jnp.zeros(l_sc.shape, F32)
    acc_sc[...] = jnp.zeros(acc_sc.shape, F32)

    def step(j, diagonal):
        s = jnp.dot(k_ref[j], qt, preferred_element_type=F32)
        if diagonal:
            key = lax.broadcasted_iota(jnp.int32, s.shape, 0)
            qry = lax.broadcasted_iota(jnp.int32, s.shape, 1)
            s = jnp.where(key <= qry, s, NEG)
        m_prev = m_sc[...]
        m_new = jnp.maximum(m_prev, jnp.max(s, axis=0, keepdims=True))
        alpha = jnp.exp2(m_prev - m_new)
        p = jnp.exp2(s - m_new)
        l_sc[...] = alpha * l_sc[...] + jnp.sum(p, axis=0, keepdims=True)
        pv = jnp.dot(vt_ref[j], p.astype(BF16), preferred_element_type=F32)
        acc_sc[...] = alpha * acc_sc[...] + pv
        m_sc[...] = m_new

    def body(j, carry):
        step(j, False)
        return carry

    lax.fori_loop(0, i, body, 0)
    step(i, True)
    o = acc_sc[...] / l_sc[...]
    o_ref[...] = o.T.astype(BF16)


def _attention(qt, k4, vt4):
    hq, _, s = qt.shape
    nb = s // ATTN_BLOCK
    bq = ATTN_BLOCK
    return pl.pallas_call(
        _attn_kernel,
        grid=(hq, nb),
        in_specs=[pl.BlockSpec((None, MLA_QK_PAD, bq), lambda h, i: (h, 0, i)),
                  pl.BlockSpec((None, nb, bq, MLA_QK_PAD), lambda h, i: (h, 0, 0, 0)),
                  pl.BlockSpec((None, nb, MLA_V, bq), lambda h, i: (h, 0, 0, 0))],
        out_specs=pl.BlockSpec((bq, MLA_V), lambda h, i: (i, h)),
        out_shape=jax.ShapeDtypeStruct((s, hq * MLA_V), BF16),
        scratch_shapes=[pltpu.VMEM((1, bq), F32), pltpu.VMEM((1, bq), F32),
                        pltpu.VMEM((MLA_V, bq), F32)],
        name="mla_attention",
        compiler_params=_params("arbitrary", "arbitrary"),
    )(qt, k4, vt4)


def _ret_gammas():
    return [1.0 - 2.0 ** (-5.0 - hh) for hh in range(RET_HEADS)]


def _ret_tables(c):
    log_gamma = jnp.log1p(-jnp.exp2(-5.0 - jnp.arange(RET_HEADS, dtype=F32)))
    idx = jnp.arange(c, dtype=F32)
    diff = idx[:, None] - idx[None, :]
    decay = jnp.where(diff >= 0, jnp.exp(log_gamma[:, None, None] * jnp.maximum(diff, 0.0)), 0.0)
    xi = jnp.exp(log_gamma[:, None] * (idx + 1.0))
    zeta = jnp.exp(log_gamma[:, None] * (c - 1.0 - idx))
    xi_b = jnp.broadcast_to(xi[:, :, None], (RET_HEADS, c, RET_QK))
    zeta_b = jnp.broadcast_to(zeta[:, :, None], (RET_HEADS, c, RET_QK))
    return decay, xi_b, zeta_b


def _ret_kernel(h_ref, w_ref, cr_ref, sr_ref, dec_ref, xi_ref, zeta_ref, g_ref, y_ref, r_sc):
    @pl.when(pl.program_id(0) == 0)
    def _():
        r_sc[...] = jnp.zeros(r_sc.shape, F32)

    c = h_ref.shape[0]
    nq = RET_HEADS * RET_QK
    nv = RET_HEADS * RET_V
    proj = jnp.dot(h_ref[...], w_ref[...], preferred_element_type=F32)
    cos = cr_ref[...]
    sin = sr_ref[...]
    half = RET_QK // 2
    gamma_c = [g ** c for g in _ret_gammas()]

    def rot(base):
        x1 = proj[:, base:base + half]
        x2 = proj[:, base + half:base + RET_QK]
        return jnp.concatenate([x1 * cos - x2 * sin, x2 * cos + x1 * sin], axis=1)

    for hh in range(RET_HEADS):
        q = rot(hh * RET_QK)
        k = rot(nq + hh * RET_QK) * (RET_QK ** -0.5)
        v = proj[:, 2 * nq + hh * RET_V:2 * nq + (hh + 1) * RET_V].astype(BF16)
        gate = proj[:, 2 * nq + nv + hh * RET_V:2 * nq + nv + (hh + 1) * RET_V]
        a = lax.dot_general(q.astype(BF16), k.astype(BF16), _TRANS_B, preferred_element_type=F32)
        a = a * dec_ref[hh]
        inner = jnp.dot(a.astype(BF16), v, preferred_element_type=F32)
        r = r_sc[hh]
        cross = jnp.dot((q * xi_ref[hh]).astype(BF16), r.astype(BF16), preferred_element_type=F32)
        kz = (k * zeta_ref[hh]).astype(BF16)
        r_sc[hh] = r * gamma_c[hh] + lax.dot_general(kz, v, _TRANS_A, preferred_element_type=F32)
        o = inner + cross
        mu = jnp.mean(o, axis=-1, keepdims=True)
        ctr = o - mu
        var = jnp.mean(ctr * ctr, axis=-1, keepdims=True)
        y = ctr * lax.rsqrt(var + EPS) * g_ref[:, hh * RET_V:(hh + 1) * RET_V] * _silu(gate)
        y_ref[:, hh * RET_V:(hh + 1) * RET_V] = y.astype(BF16)


def _retention(h, w_ret, cos_r, sin_r, g_ret):
    s, d = h.shape
    c = RET_CHUNK
    decay, xi_b, zeta_b = _ret_tables(c)
    nv = RET_HEADS * RET_V
    return pl.pallas_call(
        _ret_kernel,
        grid=(s // c,),
        in_specs=[pl.BlockSpec((c, d), lambda i: (i, 0)),
                  _resident(w_ret.shape),
                  pl.BlockSpec((c, RET_QK // 2), lambda i: (i, 0)),
                  pl.BlockSpec((c, RET_QK // 2), lambda i: (i, 0)),
                  _resident(decay.shape), _resident(xi_b.shape), _resident(zeta_b.shape),
                  _resident((1, nv))],
        out_specs=pl.BlockSpec((c, nv), lambda i: (i, 0)),
        out_shape=jax.ShapeDtypeStruct((s, nv), BF16),
        scratch_shapes=[pltpu.VMEM((RET_HEADS, RET_QK, RET_V), F32)],
        name="retention",
        compiler_params=_params("arbitrary"),
    )(h, w_ret, cos_r, sin_r, decay, xi_b, zeta_b, g_ret.reshape(1, nv))


def _final_kernel(h_ref, x_ref, ym_ref, yr_ref, wg_ref, wpm_ref, wpr_ref, wo_ref, mod_ref, gf_ref,
                  o_ref, *, last):
    d = x_ref.shape[1]
    g = jnp.dot(h_ref[...], wg_ref[...], preferred_element_type=F32)
    ym = (ym_ref[...].astype(F32) * _silu(g[:, 0:d])).astype(BF16)
    a = jnp.dot(ym, wpm_ref[...], preferred_element_type=F32)
    b = jnp.dot(yr_ref[...], wpr_ref[...], preferred_element_type=F32)
    merged = _sigmoid(g[:, d:2 * d]) * a + _sigmoid(g[:, 2 * d:3 * d]) * b
    o = jnp.dot(merged.astype(BF16), wo_ref[...], preferred_element_type=F32)
    xo = x_ref[...] + mod_ref[:, 2 * d:3 * d] * o
    if last:
        xo = _rms(xo, gf_ref[...])
    o_ref[...] = xo


def _final(h, x2, y_mla, y_ret, wg, wpm, wpr, wo, mod, g_final, last):
    s, d = x2.shape
    tm = ROW_TILE
    return pl.pallas_call(
        functools.partial(_final_kernel, last=last),
        grid=(s // tm,),
        in_specs=[pl.BlockSpec((tm, d), lambda i: (i, 0)),
                  pl.BlockSpec((tm, d), lambda i: (i, 0)),
                  pl.BlockSpec((tm, y_mla.shape[1]), lambda i: (i, 0)),
                  pl.BlockSpec((tm, y_ret.shape[1]), lambda i: (i, 0)),
                  _resident(wg.shape), _resident(wpm.shape), _resident(wpr.shape), _resident(wo.shape),
                  _resident((1, 3 * d)), _resident((1, d))],
        out_specs=pl.BlockSpec((tm, d), lambda i: (i, 0)),
        out_shape=jax.ShapeDtypeStruct((s, d), F32),
        name="final_merge",
        compiler_params=_params("arbitrary"),
    )(h, x2, y_mla, y_ret, wg, wpm, wpr, wo, mod, g_final.reshape(1, d))


def _split_w_in(w_in):
    sizes = (MLA_Q_RANK, MLA_KV_RANK, MLA_ROPE, MLA_HEADS * MLA_V,
             RET_HEADS * RET_QK, RET_HEADS * RET_QK, RET_HEADS * RET_V, RET_HEADS * RET_V,
             D_MODEL, D_MODEL)
    offs = np.cumsum((0,) + sizes)
    col = lambda a, b: w_in[:, offs[a]:offs[b]].astype(BF16)
    wc = col(0, 2)
    wpe_t = jnp.pad(col(2, 3).T, ((0, 128 - MLA_ROPE), (0, 0)))
    w_ret = col(4, 8)
    wg = jnp.concatenate([col(3, 4), col(8, 10)], axis=1)
    return wc, wpe_t, w_ret, wg


def kernel(x, c, positions, w_ada, b_ada, g_pre, w_in, g_q, w_q_up, g_kv, w_kv_up, g_ret,
           w_proj_mla, w_proj_ret, w_out, g_final):
    bsz, s, d = x.shape
    assert bsz == 1 and d == D_MODEL and s % ATTN_BLOCK == 0 and s % RET_CHUNK == 0
    depth = w_in.shape[0]
    x2 = x.reshape(s, d)
    cos_r, sin_r, cos_m, sin_m = _rope_tables(positions)
    nb = s // ATTN_BLOCK
    for l in range(depth):
        wc, wpe_t, w_ret, wg = _split_w_in(w_in[l])
        wq_t = w_q_up[l].T.astype(BF16)
        wkv = w_kv_up[l].reshape(MLA_KV_RANK, MLA_HEADS, MLA_NOPE + MLA_V)
        wk = wkv[:, :, :MLA_NOPE].reshape(MLA_KV_RANK, MLA_HEADS * MLA_NOPE).astype(BF16)
        wv_t = wkv[:, :, MLA_NOPE:].reshape(MLA_KV_RANK, MLA_HEADS * MLA_V).T.astype(BF16)
        mod = _mod(c, w_ada[l], b_ada[l])
        h = _hnorm(x2, mod, g_pre[l])
        qt, k, vt4 = _mlaprep(h, wc, wpe_t, g_q[l], g_kv[l], wq_t, wk, wv_t, cos_m, sin_m)
        y_mla = _attention(qt, k.reshape(MLA_HEADS, nb, ATTN_BLOCK, MLA_QK_PAD), vt4)
        y_ret = _retention(h, w_ret, cos_r, sin_r, g_ret[l])
        x2 = _final(h, x2, y_mla, y_ret, wg, w_proj_mla[l].astype(BF16), w_proj_ret[l].astype(BF16),
                    w_out[l].astype(BF16), mod, g_final, last=(l == depth - 1))
    return x2.reshape(bsz, s, d)
```
